```python
import jax
import jax.numpy as jnp
from jax import lax
import numpy as np

D_MODEL = 2048
BATCH = 2
SEQ = 4096
DEPTH = 2
DEC_BATCH = 128
DEC_SEQ = 4
PAST_LEN = 16384
PAGE_SIZE = 128

ROPE_THETA = 10000.0
EPS = 1e-6
NEG_INF = -1e30
MLA_HEADS = 16
MLA_Q_LORA = 512
MLA_KV_LORA = 512
MLA_NOPE = 128
MLA_ROPE = 64
MLA_QK = MLA_NOPE + MLA_ROPE
MLA_V = 128
SWA_HEADS = 32
SWA_KV_HEADS = 8
SWA_HEAD_DIM = 64
SWA_GROUP = SWA_HEADS // SWA_KV_HEADS
WINDOW = 128
D_FF = 4 * D_MODEL
N_MLA_LAYERS = (DEPTH + 1) // 2
N_SWA_LAYERS = DEPTH // 2
Q_BLOCK = 128

kernel_name = 'mla_swa_sink_hybrid_decode_step'


def rms_norm(x, g):
    xf = x.astype(jnp.float32)
    y = xf * lax.rsqrt(jnp.mean(xf * xf, axis=-1, keepdims=True) + EPS)
    return (y * g.astype(jnp.float32)).astype(x.dtype)


def rope(x, pos):
    d = x.shape[-1]
    half = d // 2
    inv_freq = ROPE_THETA ** (-(jnp.arange(half, dtype=jnp.float32) * 2.0) / d)
    ang = pos.astype(jnp.float32)[:, None] * inv_freq[None, :]
    cos = jnp.cos(ang)[:, None, :]
    sin = jnp.sin(ang)[:, None, :]
    xf = x.astype(jnp.float32)
    x1, x2 = xf[..., :half], xf[..., half:]
    return jnp.concatenate([x1 * cos - x2 * sin, x2 * cos + x1 * sin], axis=-1).astype(x.dtype)


def sq_relu_mlp(h, w_up, w_down):
    u = jax.nn.relu(jnp.einsum('bsd,df->bsf', h, w_up))
    return jnp.einsum('bsf,fd->bsd', u * u, w_down)


def mla_project(xn, w_dqkv, g_qa, g_kva, w_uq, g_qn, pos):
    a = jnp.einsum('bsd,de->bse', xn, w_dqkv)
    cq = rms_norm(a[..., :MLA_Q_LORA], g_qa)
    c_kv = rms_norm(a[..., MLA_Q_LORA:MLA_Q_LORA + MLA_KV_LORA], g_kva)
    k_pe = a[..., MLA_Q_LORA + MLA_KV_LORA:]
    q = rms_norm(jnp.einsum('bsr,rhe->bshe', cq, w_uq), g_qn)
    q = jnp.concatenate([q[..., :MLA_NOPE], rope(q[..., MLA_NOPE:], pos)], axis=-1)
    return q, c_kv, k_pe


def mla_kv(c_kv, k_pe, pos, w_ukv, g_kn):
    kv = jnp.einsum('...lr,rhe->...lhe', c_kv, w_ukv)
    k_nope, v = kv[..., :MLA_NOPE], kv[..., MLA_NOPE:]
    k_pe_b = jnp.broadcast_to(k_pe[..., None, :], k_nope.shape[:-1] + (MLA_ROPE,))
    k = rms_norm(jnp.concatenate([k_nope, k_pe_b], axis=-1), g_kn)
    k = jnp.concatenate([k[..., :MLA_NOPE], rope(k[..., MLA_NOPE:], pos)], axis=-1)
    return k, v


def mla_prompt_attn(q, k, v):
    B, S = q.shape[0], q.shape[1]
    nb = S // Q_BLOCK
    qb = q.reshape(B, nb, Q_BLOCK, MLA_HEADS, MLA_QK).transpose(1, 0, 2, 3, 4)
    kpos = jnp.arange(S)
    scale = MLA_QK ** -0.5

    def one_block(args):
        qblk, j = args
        s = jnp.einsum('bqhe,bkhe->bhqk', qblk, k).astype(jnp.float32) * scale
        qpos = j * Q_BLOCK + jnp.arange(Q_BLOCK)
        s = jnp.where((kpos[None, :] <= qpos[:, None])[None, None], s, NEG_INF)
        p = jax.nn.softmax(s, axis=-1).astype(v.dtype)
        return jnp.einsum('bhqk,bkhe->bqhe', p, v)

    o = lax.map(one_block, (qb, jnp.arange(nb)))
    return o.transpose(1, 0, 2, 3, 4).reshape(B, S, MLA_HEADS, MLA_V)


def mla_sample_attn(q, c_new, kpe_new, pool_c, pool_kpe, layer, page_table, w_ukv, g_kn):
    T = q.shape[1]
    pos = jnp.arange(PAST_LEN + T)
    qpos = PAST_LEN + jnp.arange(T)
    mask = pos[None, :] <= qpos[:, None]
    scale = MLA_QK ** -0.5

    def one_seq(args):
        qs, cs, ks, pages = args
        c = jnp.concatenate([pool_c[layer, pages].reshape(-1, MLA_KV_LORA), cs], axis=0)
        kp = jnp.concatenate([pool_kpe[layer, pages].reshape(-1, MLA_ROPE), ks], axis=0)
        k, v = mla_kv(c, kp, pos, w_ukv, g_kn)
        s = jnp.einsum('the,lhe->htl', qs, k).astype(jnp.float32) * scale
        s = jnp.where(mask[None], s, NEG_INF)
        p = jax.nn.softmax(s, axis=-1).astype(v.dtype)
        return jnp.einsum('htl,lhe->the', p, v)

    return lax.map(one_seq, (q, c_new, kpe_new, page_table))


def swa_project(xn, w_qkv, b_qkv, g_qn, g_kn, pos):
    a = jnp.einsum('bsd,de->bse', xn, w_qkv) + b_qkv
    B, S = a.shape[0], a.shape[1]
    nq = SWA_HEADS * SWA_HEAD_DIM
    nk = SWA_KV_HEADS * SWA_HEAD_DIM
    q = a[..., :nq].reshape(B, S, SWA_HEADS, SWA_HEAD_DIM)
    k = a[..., nq:nq + nk].reshape(B, S, SWA_KV_HEADS, SWA_HEAD_DIM)
    v = a[..., nq + nk:].reshape(B, S, SWA_KV_HEADS, SWA_HEAD_DIM)
    return rope(rms_norm(q, g_qn), pos), rope(rms_norm(k, g_kn), pos), v


def sink_softmax(s, sinks):
    sk = jnp.broadcast_to(sinks.astype(jnp.float32)[:, :, None, None], s.shape[:-1] + (1,))
    p = jax.nn.softmax(jnp.concatenate([s, sk], axis=-1), axis=-1)
    return p[..., :-1]


def swa_prompt_attn(q, k, v, sinks):
    B, S = q.shape[0], q.shape[1]
    nb = S // WINDOW
    qb = q.reshape(B, nb, WINDOW, SWA_KV_HEADS, SWA_GROUP, SWA_HEAD_DIM)
    kb = k.reshape(B, nb, WINDOW, SWA_KV_HEADS, SWA_HEAD_DIM)
    vb = v.reshape(B, nb, WINDOW, SWA_KV_HEADS, SWA_HEAD_DIM)
    pad = jnp.zeros_like(kb[:, :1])
    kk = jnp.concatenate([jnp.concatenate([pad, kb[:, :-1]], axis=1), kb], axis=2)
    vv = jnp.concatenate([jnp.concatenate([pad, vb[:, :-1]], axis=1), vb], axis=2)
    s = jnp.einsum('bnqgrd,bnkgd->bngrqk', qb, kk).astype(jnp.float32) * (SWA_HEAD_DIM ** -0.5)
    blk = jnp.arange(nb)[:, None] * WINDOW
    qpos = blk + jnp.arange(WINDOW)[None, :]
    kpos = blk - WINDOW + jnp.arange(2 * WINDOW)[None, :]
    diff = qpos[:, :, None] - kpos[:, None, :]
    mask = (diff >= 0) & (diff < WINDOW) & (kpos[:, None, :] >= 0)
    s = jnp.where(mask[None, :, None, None], s, NEG_INF)
    p = sink_softmax(s, sinks.reshape(SWA_KV_HEADS, SWA_GROUP)).astype(v.dtype)
    o = jnp.einsum('bngrqk,bnkgd->bnqgrd', p, vv)
    return o.reshape(B, S, SWA_HEADS * SWA_HEAD_DIM)


def swa_sample_attn(q, k_new, v_new, k_buf, v_buf, sinks):
    DB, T = q.shape[0], q.shape[1]
    kk = jnp.concatenate([k_buf, k_new], axis=1)
    vv = jnp.concatenate([v_buf, v_new], axis=1)
    qpos = PAST_LEN + jnp.arange(T)
    kpos = PAST_LEN - WINDOW + jnp.arange(WINDOW + T)
    diff = qpos[:, None] - kpos[None, :]
    mask = (diff >= 0) & (diff < WINDOW)
    qg = q.reshape(DB, T, SWA_KV_HEADS, SWA_GROUP, SWA_HEAD_DIM)
    s = jnp.einsum('bqgrd,bkgd->bgrqk', qg, kk).astype(jnp.float32) * (SWA_HEAD_DIM ** -0.5)
    s = jnp.where(mask[None, None, None], s, NEG_INF)
    p = sink_softmax(s, sinks.reshape(SWA_KV_HEADS, SWA_GROUP)).astype(v_new.dtype)
    o = jnp.einsum('bgrqk,bkgd->bqgrd', p, vv).reshape(DB, T, SWA_HEADS * SWA_HEAD_DIM)
    return o, kk[:, -WINDOW:], vv[:, -WINDOW:]


def setup_inputs(seed: int = 0) -> dict:
    key = jax.random.key(seed)
    ks = jax.random.split(key, 26)
    f32 = jnp.float32

    def w(k, shape, fan_in):
        return jax.random.normal(k, shape, f32) * (fan_in ** -0.5)

    def gain(k, shape):
        return 1.0 + 0.01 * jax.random.normal(k, shape, f32)

    n_pages = PAST_LEN // PAGE_SIZE
    n_used = DEC_BATCH * n_pages
    n_phys = n_used + n_used // 4
    page_table = jax.random.permutation(ks[0], n_phys)[:n_used].reshape(DEC_BATCH, n_pages).astype(jnp.int32)
    swa_cols = (SWA_HEADS + 2 * SWA_KV_HEADS) * SWA_HEAD_DIM
    return {
        'x_prompt': jax.random.normal(ks[1], (BATCH, SEQ, D_MODEL), f32),
        'x_sample': jax.random.normal(ks[2], (DEC_BATCH, DEC_SEQ, D_MODEL), f32),
        'cache_mla_latent': jax.random.normal(ks[3], (N_MLA_LAYERS, n_phys, PAGE_SIZE, MLA_KV_LORA), f32),
        'cache_mla_rope': jax.random.normal(ks[4], (N_MLA_LAYERS, n_phys, PAGE_SIZE, MLA_ROPE), f32),
        'page_table': page_table,
        'state_swa_k': jax.random.normal(ks[5], (N_SWA_LAYERS, DEC_BATCH, WINDOW, SWA_KV_HEADS, SWA_HEAD_DIM), f32),
        'state_swa_v': jax.random.normal(ks[6], (N_SWA_LAYERS, DEC_BATCH, WINDOW, SWA_KV_HEADS, SWA_HEAD_DIM), f32),
        'norm_mix': gain(ks[7], (DEPTH, D_MODEL)),
        'norm_mlp': gain(ks[8], (DEPTH, D_MODEL)),
        'mla_w_dqkv': w(ks[9], (N_MLA_LAYERS, D_MODEL, MLA_Q_LORA + MLA_KV_LORA + MLA_ROPE), D_MODEL),
        'mla_g_qa': gain(ks[10], (N_MLA_LAYERS, MLA_Q_LORA)),
        'mla_g_kva': gain(ks[11], (N_MLA_LAYERS, MLA_KV_LORA)),
        'mla_w_uq': w(ks[12], (N_MLA_LAYERS, MLA_Q_LORA, MLA_HEADS, MLA_QK), MLA_Q_LORA),
        'mla_w_ukv': w(ks[13], (N_MLA_LAYERS, MLA_KV_LORA, MLA_HEADS, MLA_NOPE + MLA_V), MLA_KV_LORA),
        'mla_g_qn': gain(ks[14], (N_MLA_LAYERS, MLA_QK)),
        'mla_g_kn': gain(ks[15], (N_MLA_LAYERS, MLA_QK)),
        'mla_w_o': w(ks[16], (N_MLA_LAYERS, MLA_HEADS, MLA_V, D_MODEL), MLA_HEADS * MLA_V),
        'swa_w_qkv': w(ks[17], (N_SWA_LAYERS, D_MODEL, swa_cols), D_MODEL),
        'swa_b_qkv': 0.01 * jax.random.normal(ks[18], (N_SWA_LAYERS, swa_cols), f32),
        'swa_g_qn': gain(ks[19], (N_SWA_LAYERS, SWA_HEAD_DIM)),
        'swa_g_kn': gain(ks[20], (N_SWA_LAYERS, SWA_HEAD_DIM)),
        'swa_sinks': 0.5 * jax.random.normal(ks[21], (N_SWA_LAYERS, SWA_HEADS), f32),
        'swa_w_o': w(ks[22], (N_SWA_LAYERS, SWA_HEADS * SWA_HEAD_DIM, D_MODEL), SWA_HEADS * SWA_HEAD_DIM),
        'swa_b_o': 0.01 * jax.random.normal(ks[23], (N_SWA_LAYERS, D_MODEL), f32),
        'mlp_w_up': w(ks[24], (DEPTH, D_MODEL, D_FF), D_MODEL),
        'mlp_w_down': w(ks[25], (DEPTH, D_FF, D_MODEL), D_FF),
    }


def reference(x_prompt, x_sample, cache_mla_latent, cache_mla_rope, page_table, state_swa_k, state_swa_v,
              norm_mix, norm_mlp, mla_w_dqkv, mla_g_qa, mla_g_kva, mla_w_uq, mla_w_ukv, mla_g_qn, mla_g_kn,
              mla_w_o, swa_w_qkv, swa_b_qkv, swa_g_qn, swa_g_kn, swa_sinks, swa_w_o, swa_b_o,
              mlp_w_up, mlp_w_down):
    pos_p = jnp.arange(x_prompt.shape[1])
    pos_s = PAST_LEN + jnp.arange(x_sample.shape[1])
    xp, xs = x_prompt, x_sample
    lat_p, rope_p, lat_s, rope_s = [], [], [], []
    swk_p, swv_p, swk_s, swv_s = [], [], [], []
    for i in range(DEPTH):
        j = i // 2
        hp = rms_norm(xp, norm_mix[i])
        hs = rms_norm(xs, norm_mix[i])
        if i % 2 == 0:
            qp, cp, kpe_p = mla_project(hp, mla_w_dqkv[j], mla_g_qa[j], mla_g_kva[j], mla_w_uq[j], mla_g_qn[j], pos_p)
            kp, vp = mla_kv(cp, kpe_p, pos_p, mla_w_ukv[j], mla_g_kn[j])
            op = mla_prompt_attn(qp, kp, vp)
            qs, cs, kpe_s = mla_project(hs, mla_w_dqkv[j], mla_g_qa[j], mla_g_kva[j], mla_w_uq[j], mla_g_qn[j], pos_s)
            os_ = mla_sample_attn(qs, cs, kpe_s, cache_mla_latent, cache_mla_rope, j, page_table,
                                  mla_w_ukv[j], mla_g_kn[j])
            xp = xp + jnp.einsum('bshe,hed->bsd', op, mla_w_o[j])
            xs = xs + jnp.einsum('bshe,hed->bsd', os_, mla_w_o[j])
            lat_p.append(cp)
            rope_p.append(kpe_p)
            lat_s.append(cs)
            rope_s.append(kpe_s)
        else:
            qp, kp, vp = swa_project(hp, swa_w_qkv[j], swa_b_qkv[j], swa_g_qn[j], swa_g_kn[j], pos_p)
            op = swa_prompt_attn(qp, kp, vp, swa_sinks[j])
            qs, ks_, vs_ = swa_project(hs, swa_w_qkv[j], swa_b_qkv[j], swa_g_qn[j], swa_g_kn[j], pos_s)
            os_, kbuf, vbuf = swa_sample_attn(qs, ks_, vs_, state_swa_k[j], state_swa_v[j], swa_sinks[j])
            xp = xp + jnp.einsum('bse,ed->bsd', op, swa_w_o[j]) + swa_b_o[j]
            xs = xs + jnp.einsum('bse,ed->bsd', os_, swa_w_o[j]) + swa_b_o[j]
            swk_p.append(kp[:, -WINDOW:])
            swv_p.append(vp[:, -WINDOW:])
            swk_s.append(kbuf)
            swv_s.append(vbuf)
        xp = xp + sq_relu_mlp(rms_norm(xp, norm_mlp[i]), mlp_w_up[i], mlp_w_down[i])
        xs = xs + sq_relu_mlp(rms_norm(xs, norm_mlp[i]), mlp_w_up[i], mlp_w_down[i])
    return (xp, xs, jnp.stack(lat_p), jnp.stack(rope_p), jnp.stack(lat_s), jnp.stack(rope_s),
            jnp.stack(swk_p), jnp.stack(swv_p), jnp.stack(swk_s), jnp.stack(swv_s))
```

```python
import functools

import jax
import jax.numpy as jnp
from jax import lax
from jax.experimental import pallas as pl
from jax.experimental.pallas import tpu as pltpu

EPS = 1e-6
ROPE_THETA = 10000.0
NEG_INF = -1e30
F32 = jnp.float32
BF16 = jnp.bfloat16

LANES = 128
VMEM_LIMIT_BYTES = 48 * 1024 * 1024
ROW_TILE = 512
ROPE_DIM = 64
NT_DIMS = (((1,), (1,)), ((), ()))


def _cparams(*sem):
    return pltpu.CompilerParams(dimension_semantics=sem, vmem_limit_bytes=VMEM_LIMIT_BYTES)


def _rms(x, g):
    return x * lax.rsqrt(jnp.mean(x * x, axis=-1, keepdims=True) + EPS) * g


def _lane_iota(shape):
    return lax.broadcasted_iota(jnp.int32, shape, len(shape) - 1)


def _rope_pair(x, cos, sin):
    first_half = (_lane_iota(x.shape) & (ROPE_DIM // 2)) == 0
    rot = jnp.where(first_half, pltpu.roll(x, LANES - ROPE_DIM // 2, 1), pltpu.roll(x, ROPE_DIM // 2, 1))
    return x * cos + rot * sin


def _pair_sumsq(x):
    sq = x * x
    lo = _lane_iota(x.shape) < ROPE_DIM
    return (jnp.sum(jnp.where(lo, sq, 0.0), axis=-1, keepdims=True),
            jnp.sum(jnp.where(lo, 0.0, sq), axis=-1, keepdims=True))


def _rope_tables(pos):
    half = ROPE_DIM // 2
    inv_freq = ROPE_THETA ** (-(jnp.arange(half, dtype=F32) * 2.0) / ROPE_DIM)
    ang = pos.astype(F32)[:, None] * inv_freq[None, :]
    c, s = jnp.cos(ang), jnp.sin(ang)
    return jnp.tile(jnp.concatenate([c, c], -1), (1, 2)), jnp.tile(jnp.concatenate([-s, s], -1), (1, 2))


def _mla_down_kernel(x_ref, gm_ref, wq_ref, wkv_ref, wpe_ref, gqa_ref, gkva_ref, cq_ref, ckv_ref, kpe_ref):
    xn = _rms(x_ref[...], gm_ref[...]).astype(BF16)
    cq_ref[...] = _rms(jnp.dot(xn, wq_ref[...], preferred_element_type=F32), gqa_ref[...]).astype(BF16)
    ckv_ref[...] = _rms(jnp.dot(xn, wkv_ref[...], preferred_element_type=F32), gkva_ref[...])
    kpe_ref[...] = jnp.dot(xn, wpe_ref[...], preferred_element_type=F32)


def _mla_down(x, g_mix, wq, wkv, wpe2, g_qa, g_kva):
    m, d = x.shape
    tm = min(ROW_TILE, m)
    row = lambda i: (i, 0)
    fixed = lambda i: (0, 0)
    return pl.pallas_call(
        _mla_down_kernel,
        grid=(m // tm,),
        in_specs=[pl.BlockSpec((tm, d), row), pl.BlockSpec((1, d), fixed),
                  pl.BlockSpec(wq.shape, fixed), pl.BlockSpec(wkv.shape, fixed), pl.BlockSpec(wpe2.shape, fixed),
                  pl.BlockSpec((1, wq.shape[1]), fixed), pl.BlockSpec((1, wkv.shape[1]), fixed)],
        out_specs=[pl.BlockSpec((tm, wq.shape[1]), row), pl.BlockSpec((tm, wkv.shape[1]), row),
                   pl.BlockSpec((tm, wpe2.shape[1]), row)],
        out_shape=[jax.ShapeDtypeStruct((m, wq.shape[1]), BF16), jax.ShapeDtypeStruct((m, wkv.shape[1]), F32),
                   jax.ShapeDtypeStruct((m, wpe2.shape[1]), F32)],
        compiler_params=_cparams("parallel"),
        name="mla_down",
    )(x, g_mix, wq, wkv, wpe2, g_qa, g_kva)


def _mla_q_kernel(cq_ref, w_ref, gn_ref, gp_ref, cos_ref, sin_ref, q_ref, *, n_heads, nope, scale):
    q = jnp.dot(cq_ref[...], w_ref[...], preferred_element_type=F32)
    pe0 = n_heads * nope
    qk_dim = nope + ROPE_DIM
    cos, sin = cos_ref[...], sin_ref[...]
    lo = _lane_iota((q.shape[0], LANES)) < ROPE_DIM
    for j in range(n_heads // 2):
        pe = q[:, pe0 + j * LANES: pe0 + (j + 1) * LANES]
        ss_pe = _pair_sumsq(pe)
        r = []
        for k in range(2):
            h = 2 * j + k
            qn = q[:, h * nope:(h + 1) * nope]
            ss = jnp.sum(qn * qn, axis=-1, keepdims=True) + ss_pe[k]
            r.append(lax.rsqrt(ss / qk_dim + EPS) * scale)
            q_ref[h, :, 0:nope] = (qn * r[k] * gn_ref[...]).astype(q_ref.dtype)
        pe = _rope_pair(pe * jnp.where(lo, r[0], r[1]) * gp_ref[...], cos, sin)
        q_ref[2 * j, :, nope:nope + LANES] = jnp.where(lo, pe, 0.0).astype(q_ref.dtype)
        q_ref[2 * j + 1, :, nope:nope + LANES] = jnp.where(lo, pltpu.roll(pe, ROPE_DIM, 1), 0.0).astype(q_ref.dtype)


def _mla_q(cq, w_r, g_nope, g_pe2, cos, sin, n_heads, nope, scale, out_dtype):
    m, r = cq.shape
    tm = min(ROW_TILE, m)
    n_pos_blocks = cos.shape[0] // tm
    row = lambda i: (i, 0)
    fixed = lambda i: (0, 0)
    pos = lambda i: (i % n_pos_blocks, 0)
    return pl.pallas_call(
        functools.partial(_mla_q_kernel, n_heads=n_heads, nope=nope, scale=scale),
        grid=(m // tm,),
        in_specs=[pl.BlockSpec((tm, r), row), pl.BlockSpec(w_r.shape, fixed),
                  pl.BlockSpec((1, nope), fixed), pl.BlockSpec((1, LANES), fixed),
                  pl.BlockSpec((tm, LANES), pos), pl.BlockSpec((tm, LANES), pos)],
        out_specs=pl.BlockSpec((n_heads, tm, nope + LANES), lambda i: (0, i, 0)),
        out_shape=jax.ShapeDtypeStruct((n_heads, m, nope + LANES), out_dtype),
        compiler_params=_cparams("parallel"),
        name="mla_q",
    )(cq, w_r, g_nope, g_pe2, cos, sin)


def _mla_kv_kernel(c_ref, kpe_ref, w_ref, gn_ref, gp_ref, cos_ref, sin_ref, k_ref, v_ref, *, n_heads, nope):
    kv = jnp.dot(c_ref[...].astype(BF16), w_ref[...], preferred_element_type=F32)
    v0 = n_heads * nope
    vdim = (kv.shape[1] - v0) // n_heads
    kpe = kpe_ref[...]
    cos, sin = cos_ref[...], sin_ref[...]
    lo = _lane_iota(kpe.shape) < ROPE_DIM
    ss_pe = _pair_sumsq(kpe)[0]
    for h in range(n_heads):
        kn = kv[:, h * nope:(h + 1) * nope]
        r = lax.rsqrt((jnp.sum(kn * kn, axis=-1, keepdims=True) + ss_pe) / (nope + ROPE_DIM) + EPS)
        k_ref[h, :, 0:nope] = (kn * r * gn_ref[...]).astype(k_ref.dtype)
        pe = _rope_pair(kpe * r * gp_ref[...], cos, sin)
        k_ref[h, :, nope:nope + LANES] = jnp.where(lo, pe, 0.0).astype(k_ref.dtype)
        v_ref[h] = kv[:, v0 + h * vdim: v0 + (h + 1) * vdim].astype(v_ref.dtype)


def _mla_kv(c, kpe2, w_r, g_nope, g_pe2, cos, sin, n_heads, nope, vdim):
    m, r = c.shape
    tm = min(ROW_TILE, m)
    n_pos_blocks = cos.shape[0] // tm
    row = lambda i: (i, 0)
    fixed = lambda i: (0, 0)
    pos = lambda i: (i % n_pos_blocks, 0)
    return pl.pallas_call(
        functools.partial(_mla_kv_kernel, n_heads=n_heads, nope=nope),
        grid=(m // tm,),
        in_specs=[pl.BlockSpec((tm, r), row), pl.BlockSpec((tm, LANES), row), pl.BlockSpec(w_r.shape, fixed),
                  pl.BlockSpec((1, nope), fixed), pl.BlockSpec((1, LANES), fixed),
                  pl.BlockSpec((tm, LANES), pos), pl.BlockSpec((tm, LANES), pos)],
        out_specs=[pl.BlockSpec((n_heads, tm, nope + LANES), lambda i: (0, i, 0)),
                   pl.BlockSpec((n_heads, tm, vdim), lambda i: (0, i, 0))],
        out_shape=[jax.ShapeDtypeStruct((n_heads, m, nope + LANES), BF16),
                   jax.ShapeDtypeStruct((n_heads, m, vdim), BF16)],
        compiler_params=_cparams("parallel"),
        name="mla_kv",
    )(c, kpe2, w_r, g_nope, g_pe2, cos, sin)


def _mla_flash_kernel(q_ref, k_ref, v_ref, o_ref, m_ref, l_ref, acc_ref, *, tq, tk):
    i = pl.program_id(2)
    q = q_ref[0]
    m_ref[...] = jnp.full(m_ref.shape, NEG_INF, F32)
    l_ref[...] = jnp.zeros(l_ref.shape, F32)
    acc_ref[...] = jnp.zeros(acc_ref.shape, F32)
    qpos = i * tq + lax.broadcasted_iota(jnp.int32, (tq, tk), 0)
    kidx = lax.broadcasted_iota(jnp.int32, (tq, tk), 1)

    def step(j, carry):
        start = pl.multiple_of(j * tk, tk)
        k = k_ref[0, pl.ds(start, tk), :]
        v = v_ref[0, pl.ds(start, tk), :]
        s = lax.dot_general(q, k, NT_DIMS, preferred_element_type=F32)
        s = jnp.where(kidx + j * tk <= qpos, s, NEG_INF)
        m_prev = m_ref[...]
        m_new = jnp.maximum(m_prev, jnp.max(s, axis=-1, keepdims=True))
        alpha = jnp.exp(m_prev - m_new)
        p = jnp.exp(s - m_new)
        l_ref[...] = alpha * l_ref[...] + jnp.sum(p, axis=-1, keepdims=True)
        acc_ref[...] = alpha * acc_ref[...] + jnp.dot(p.astype(BF16), v, preferred_element_type=F32)
        m_ref[...] = m_new
        return carry

    lax.fori_loop(0, ((i + 1) * tq + tk - 1) // tk, step, 0)
    o_ref[...] = (acc_ref[...] / l_ref[...]).astype(o_ref.dtype)


def _mla_flash(q, k, v, batch, seq):
    n_heads, m, qk = q.shape
    vdim = v.shape[2]
    tq = tk = min(ROW_TILE, seq)
    nq = seq // tq
    return pl.pallas_call(
        functools.partial(_mla_flash_kernel, tq=tq, tk=tk),
        grid=(batch, n_heads, nq),
        in_specs=[pl.BlockSpec((1, tq, qk), lambda b, h, i: (h, b * nq + i, 0)),
                  pl.BlockSpec((1, seq, qk), lambda b, h, i: (h, b, 0)),
                  pl.BlockSpec((1, seq, vdim), lambda b, h, i: (h, b, 0))],
        out_specs=pl.BlockSpec((tq, vdim), lambda b, h, i: (b * nq + i, h)),
        out_shape=jax.ShapeDtypeStruct((m, n_heads * vdim), BF16),
        scratch_shapes=[pltpu.VMEM((tq, 1), F32), pltpu.VMEM((tq, 1), F32), pltpu.VMEM((tq, vdim), F32)],
        compiler_params=_cparams("parallel", "parallel", "arbitrary"),
        name="mla_flash",
    )(q, k, v)


def _mla_sample_kernel(pt_ref, *refs, pps, n_pages, n_heads, nope, tokens, page):
    lat_refs, rope_refs = refs[:pps], refs[pps:2 * pps]
    (wukt_ref, qabs_ref, qpe_ref, qpes_ref, cnew_ref, kpenew_ref, cost_ref, sint_ref, pcos_ref, psin_ref,
     gpe_ref, eye_ref, o_ref, lhs_ref, cbf_ref, kpe_ref, m_ref, l_ref, acc_ref) = refs[2 * pps:]
    j = pl.program_id(1)
    n_rows = tokens * n_heads
    qk_dim = nope + ROPE_DIM

    @pl.when(j == 0)
    def _():
        lhs_ref[0:n_rows, :] = qabs_ref[0]
        lhs_ref[n_rows:, :] = wukt_ref[...]
        m_ref[...] = jnp.full(m_ref.shape, NEG_INF, F32)
        l_ref[...] = jnp.zeros(l_ref.shape, F32)
        acc_ref[...] = jnp.zeros(acc_ref.shape, F32)

    def attend(n, first_page, causal):
        cb = cbf_ref[0:n, :]
        res = lax.dot_general(lhs_ref[...], cb, NT_DIMS, preferred_element_type=F32)
        kt = res[n_rows:, :]
        nsq = jnp.sum((kt * kt).reshape(n_heads, nope, n), axis=1)
        kpe = kpe_ref[0:n, :]
        hi = kpe.astype(BF16)
        lo = (kpe - hi.astype(F32)).astype(BF16)
        eye = eye_ref[...]
        kpet = (lax.dot_general(eye, hi, NT_DIMS, preferred_element_type=F32)
                + lax.dot_general(eye, lo, NT_DIMS, preferred_element_type=F32))
        r = lax.rsqrt((nsq + jnp.sum(kpet * kpet, axis=0, keepdims=True)) / qk_dim + EPS)
        z = kpet * gpe_ref[...]
        zs = jnp.concatenate([z[ROPE_DIM // 2:], z[:ROPE_DIM // 2]], axis=0)
        qpe, qpes = qpe_ref[0], qpes_ref[0]
        s_pe = []
        for c in range(n // page):
            cols = slice(c * page, (c + 1) * page)
            kr = (z[:, cols] * cost_ref[...] + zs[:, cols] * sint_ref[...]).astype(BF16)
            p_idx = first_page + c
            qr = (qpe * pcos_ref[pl.ds(p_idx, 1), :] + qpes * psin_ref[pl.ds(p_idx, 1), :]).astype(BF16)
            s_pe.append(jnp.dot(qr, kr, preferred_element_type=F32))
        s_pe = s_pe[0] if len(s_pe) == 1 else jnp.concatenate(s_pe, axis=1)
        s = (res[0:n_rows, :] + s_pe) * jnp.concatenate([r] * tokens, axis=0)
        if causal:
            t_row = lax.broadcasted_iota(jnp.int32, s.shape, 0) // n_heads
            s = jnp.where(lax.broadcasted_iota(jnp.int32, s.shape, 1) <= t_row, s, NEG_INF)
        m_prev = m_ref[...]
        m_new = jnp.maximum(m_prev, jnp.max(s, axis=-1, keepdims=True))
        alpha = jnp.exp(m_prev - m_new)
        p = jnp.exp(s - m_new)
        l_ref[...] = alpha * l_ref[...] + jnp.sum(p, axis=-1, keepdims=True)
        acc_ref[...] = alpha * acc_ref[...] + jnp.dot(p.astype(BF16), cb, preferred_element_type=F32)
        m_ref[...] = m_new

    for k in range(pps):
        cbf_ref[k * page:(k + 1) * page, :] = lat_refs[k][0, 0].astype(BF16)
        kpe_ref[k * page:(k + 1) * page, :] = rope_refs[k][0, 0]
    attend(pps * page, j * pps, False)

    @pl.when(j == pl.num_programs(1) - 1)
    def _():
        n_new = cnew_ref.shape[1]
        cbf_ref[0:page, :] = jnp.zeros((page, cbf_ref.shape[1]), BF16)
        cbf_ref[0:n_new, :] = cnew_ref[0].astype(BF16)
        kpe_ref[0:page, :] = jnp.zeros((page, kpe_ref.shape[1]), F32)
        kpe_ref[0:n_new, :] = kpenew_ref[0]
        attend(page, n_pages, True)
        o_ref[0] = acc_ref[...] / l_ref[...]


def _mla_sample_attn(page_table, cache_lat, cache_rope, layer, wukt, qabs, qpe, qpes, c_new, kpe_new,
                     cos_t, sin_t, pcos, psin, gpe_col, n_heads, nope, tokens):
    dec_batch, n_pages = page_table.shape
    page, lora = cache_lat.shape[2], cache_lat.shape[3]
    pps = min(8, n_pages)
    n_steps = n_pages // pps
    n_rows = tokens * n_heads
    eye = jnp.eye(ROPE_DIM, dtype=BF16)

    def page_map(k):
        return lambda b, j, pt: (layer, pt[b * n_pages + j * pps + k], 0, 0)

    fixed2 = lambda b, j, pt: (0, 0)
    per_seq = lambda b, j, pt: (b, 0, 0)
    in_specs = ([pl.BlockSpec((1, 1, page, lora), page_map(k)) for k in range(pps)]
                + [pl.BlockSpec((1, 1, page, ROPE_DIM), page_map(k)) for k in range(pps)]
                + [pl.BlockSpec(wukt.shape, fixed2),
                   pl.BlockSpec((1, n_rows, lora), per_seq),
                   pl.BlockSpec((1, n_rows, ROPE_DIM), per_seq), pl.BlockSpec((1, n_rows, ROPE_DIM), per_seq),
                   pl.BlockSpec((1,) + c_new.shape[1:], per_seq), pl.BlockSpec((1,) + kpe_new.shape[1:], per_seq),
                   pl.BlockSpec(cos_t.shape, fixed2), pl.BlockSpec(sin_t.shape, fixed2),
                   pl.BlockSpec(pcos.shape, fixed2), pl.BlockSpec(psin.shape, fixed2),
                   pl.BlockSpec(gpe_col.shape, fixed2), pl.BlockSpec(eye.shape, fixed2)])
    grid_spec = pltpu.PrefetchScalarGridSpec(
        num_scalar_prefetch=1,
        grid=(dec_batch, n_steps),
        in_specs=in_specs,
        out_specs=pl.BlockSpec((1, n_rows, lora), per_seq),
        scratch_shapes=[pltpu.VMEM((n_rows + wukt.shape[0], lora), BF16),
                        pltpu.VMEM((pps * page, lora), BF16),
                        pltpu.VMEM((pps * page, ROPE_DIM), F32),
                        pltpu.VMEM((n_rows, 1), F32), pltpu.VMEM((n_rows, 1), F32),
                        pltpu.VMEM((n_rows, lora), F32)])
    return pl.pallas_call(
        functools.partial(_mla_sample_kernel, pps=pps, n_pages=n_pages, n_heads=n_heads, nope=nope,
                          tokens=tokens, page=page),
        grid_spec=grid_spec,
        out_shape=jax.ShapeDtypeStruct((dec_batch, n_rows, lora), F32),
        compiler_params=_cparams("parallel", "arbitrary"),
        name="mla_sample_attn",
    )(page_table.reshape(-1), *([cache_lat] * pps), *([cache_rope] * pps), wukt, qabs, qpe, qpes, c_new, kpe_new,
      cos_t, sin_t, pcos, psin, gpe_col, eye)


def _bmm_kernel(a_ref, w_ref, o_ref):
    o_ref[0] = jnp.dot(a_ref[0], w_ref[0], preferred_element_type=F32).astype(o_ref.dtype)


def _bmm(a, w, out_dtype):
    n_heads, t, k = a.shape
    n = w.shape[2]
    return pl.pallas_call(
        _bmm_kernel,
        grid=(n_heads,),
        in_specs=[pl.BlockSpec((1, t, k), lambda h: (h, 0, 0)), pl.BlockSpec((1, k, n), lambda h: (h, 0, 0))],
        out_specs=pl.BlockSpec((1, t, n), lambda h: (h, 0, 0)),
        out_shape=jax.ShapeDtypeStruct((n_heads, t, n), out_dtype),
        compiler_params=_cparams("parallel"),
        name="head_bmm",
    )(a, w)


def _proj_res_kernel(a_ref, w_ref, res_ref, o_ref):
    o_ref[...] = res_ref[...] + jnp.dot(a_ref[...], w_ref[...], preferred_element_type=F32)


def _proj_res_bias_kernel(a_ref, w_ref, b_ref, res_ref, o_ref):
    o_ref[...] = res_ref[...] + jnp.dot(a_ref[...], w_ref[...], preferred_element_type=F32) + b_ref[...]


def _proj_res(a, w, res, bias=None):
    m, k = a.shape
    n = w.shape[1]
    tm, tn = min(ROW_TILE, m), min(1024, n)
    a_spec = pl.BlockSpec((tm, k), lambda i, j: (i, 0))
    w_spec = pl.BlockSpec((k, tn), lambda i, j: (0, j))
    o_spec = pl.BlockSpec((tm, tn), lambda i, j: (i, j))
    if bias is None:
        kern, specs, args = _proj_res_kernel, [a_spec, w_spec, o_spec], (a, w, res)
    else:
        kern, specs = _proj_res_bias_kernel, [a_spec, w_spec, pl.BlockSpec((1, tn), lambda i, j: (0, j)), o_spec]
        args = (a, w, bias, res)
    return pl.pallas_call(
        kern, grid=(m // tm, n // tn), in_specs=specs, out_specs=o_spec,
        out_shape=jax.ShapeDtypeStruct((m, n), F32),
        compiler_params=_cparams("parallel", "parallel"),
        name="proj_res",
    )(*args)


def _mlp_kernel(x_ref, g_ref, wu_ref, wd_ref, o_ref, xn_ref):
    @pl.when(pl.program_id(1) == 0)
    def _():
        x = x_ref[...]
        xn_ref[...] = _rms(x, g_ref[...]).astype(BF16)
        o_ref[...] = x

    u = jnp.maximum(jnp.dot(xn_ref[...], wu_ref[...], preferred_element_type=F32), 0.0)
    o_ref[...] += jnp.dot((u * u).astype(BF16), wd_ref[...], preferred_element_type=F32)


def _mlp(x, g, w_up, w_down):
    m, d = x.shape
    f = w_up.shape[1]
    tm, tf = min(ROW_TILE, m), 512
    return pl.pallas_call(
        _mlp_kernel,
        grid=(m // tm, f // tf),
        in_specs=[pl.BlockSpec((tm, d), lambda i, j: (i, 0)), pl.BlockSpec((1, d), lambda i, j: (0, 0)),
                  pl.BlockSpec((d, tf), lambda i, j: (0, j)), pl.BlockSpec((tf, d), lambda i, j: (j, 0))],
        out_specs=pl.BlockSpec((tm, d), lambda i, j: (i, 0)),
        out_shape=jax.ShapeDtypeStruct((m, d), F32),
        scratch_shapes=[pltpu.VMEM((tm, d), BF16)],
        compiler_params=_cparams("parallel", "arbitrary"),
        name="mlp",
    )(x, g, w_up, w_down)


def _swa_proj_kernel(x_ref, g_ref, w_ref, b_ref, gh_ref, cos_ref, sin_ref, o_ref, xn_ref, *, rope_pairs, scale):
    @pl.when(pl.program_id(1) == 0)
    def _():
        xn_ref[...] = _rms(x_ref[...], g_ref[...]).astype(BF16)

    a = jnp.dot(xn_ref[...], w_ref[...], preferred_element_type=F32) + b_ref[...]
    cos, sin = cos_ref[...], sin_ref[...]
    lo = _lane_iota((a.shape[0], LANES)) < ROPE_DIM
    for c in range(a.shape[1] // LANES):
        x = a[:, c * LANES:(c + 1) * LANES]
        if c < rope_pairs:
            ss = _pair_sumsq(x)
            r = jnp.where(lo, lax.rsqrt(ss[0] / ROPE_DIM + EPS), lax.rsqrt(ss[1] / ROPE_DIM + EPS))
            x = _rope_pair(x * r * gh_ref[...], cos, sin) * scale
        o_ref[:, c * LANES:(c + 1) * LANES] = x.astype(o_ref.dtype)


def _swa_proj(x, g_mix, w, b, g_head2, cos, sin, rope_pairs, scale, out_dtype):
    m, d = x.shape
    n = w.shape[1]
    tm, tn = min(ROW_TILE, m), 1024
    n_pos_blocks = cos.shape[0] // tm
    pos = lambda i, j: (i % n_pos_blocks, 0)
    return pl.pallas_call(
        functools.partial(_swa_proj_kernel, rope_pairs=rope_pairs, scale=scale),
        grid=(m // tm, n // tn),
        in_specs=[pl.BlockSpec((tm, d), lambda i, j: (i, 0)), pl.BlockSpec((1, d), lambda i, j: (0, 0)),
                  pl.BlockSpec((d, tn), lambda i, j: (0, j)), pl.BlockSpec((1, tn), lambda i, j: (0, j)),
                  pl.BlockSpec((1, LANES), lambda i, j: (0, 0)),
                  pl.BlockSpec((tm, LANES), pos), pl.BlockSpec((tm, LANES), pos)],
        out_specs=pl.BlockSpec((tm, tn), lambda i, j: (i, j)),
        out_shape=jax.ShapeDtypeStruct((m, n), out_dtype),
        scratch_shapes=[pltpu.VMEM((tm, d), BF16)],
        compiler_params=_cparams("parallel", "arbitrary"),
        name="swa_proj",
    )(x, g_mix, w, b, g_head2, cos, sin)


def _sink_softmax(s, sink):
    m = jnp.maximum(jnp.max(s, axis=-1, keepdims=True), sink)
    p = jnp.exp(s - m)
    return p / (jnp.sum(p, axis=-1, keepdims=True) + jnp.exp(sink - m))


def _swa_prompt_kernel(sinks_ref, q_ref, kvp_ref, kvc_ref, o_ref, *, kv_heads, group, window):
    n = pl.program_id(1)
    hd = ROPE_DIM
    v0 = kv_heads * hd
    rows = group * window
    qi = lax.broadcasted_iota(jnp.int32, (rows, 2 * window), 0) % window
    kj = lax.broadcasted_iota(jnp.int32, (rows, 2 * window), 1)
    diff = qi + window - kj
    mask = (diff >= 0) & (diff < window) & ((kj >= window) | (n > 0))
    for g in range(kv_heads):
        kk = jnp.concatenate([kvp_ref[:, g * hd:(g + 1) * hd], kvc_ref[:, g * hd:(g + 1) * hd]], axis=0).astype(BF16)
        vv = jnp.concatenate([kvp_ref[:, v0 + g * hd: v0 + (g + 1) * hd],
                              kvc_ref[:, v0 + g * hd: v0 + (g + 1) * hd]], axis=0).astype(BF16)
        q = jnp.concatenate([q_ref[:, (g * group + r) * hd:(g * group + r + 1) * hd] for r in range(group)], axis=0)
        s = jnp.where(mask, lax.dot_general(q, kk, NT_DIMS, preferred_element_type=F32), NEG_INF)
        sink = jnp.concatenate([jnp.full((window, 1), sinks_ref[g * group + r], F32) for r in range(group)], axis=0)
        o = jnp.dot(_sink_softmax(s, sink).astype(BF16), vv, preferred_element_type=F32)
        for r in range(group):
            h = g * group + r
            o_ref[:, h * hd:(h + 1) * hd] = o[r * window:(r + 1) * window].astype(o_ref.dtype)


def _swa_prompt_attn(q, kv, sinks, batch, seq, kv_heads, group, window):
    m, nq = q.shape
    nb = seq // window
    return pl.pallas_call(
        functools.partial(_swa_prompt_kernel, kv_heads=kv_heads, group=group, window=window),
        grid=(batch, nb),
        in_specs=[pl.BlockSpec(memory_space=pltpu.SMEM),
                  pl.BlockSpec((window, nq), lambda b, n: (b * nb + n, 0)),
                  pl.BlockSpec((window, kv.shape[1]), lambda b, n: (b * nb + jnp.maximum(n - 1, 0), 0)),
                  pl.BlockSpec((window, kv.shape[1]), lambda b, n: (b * nb + n, 0))],
        out_specs=pl.BlockSpec((window, nq), lambda b, n: (b * nb + n, 0)),
        out_shape=jax.ShapeDtypeStruct((m, nq), BF16),
        compiler_params=_cparams("parallel", "parallel"),
        name="swa_prompt_attn",
    )(sinks, q, kv, kv)


def _swa_sample_kernel(sinks_ref, q_ref, kvn_ref, sk_ref, sv_ref, o_ref, *, kv_heads, group, window, tokens, seqs):
    hd = ROPE_DIM
    v0 = kv_heads * hd
    tpad = q_ref.shape[1]
    rows = group * tpad
    t_row = lax.broadcasted_iota(jnp.int32, (rows, window + tpad), 0) % tpad
    kj = lax.broadcasted_iota(jnp.int32, (rows, window + tpad), 1)
    mask = ((kj < window) & (kj > t_row)) | ((kj >= window) & (kj - window <= t_row) & (kj - window < tokens))

    def one_seq(b, carry):
        for g in range(kv_heads):
            kk = jnp.concatenate([sk_ref[b, :, g * hd:(g + 1) * hd], kvn_ref[b, :, g * hd:(g + 1) * hd]],
                                 axis=0).astype(BF16)
            vv = jnp.concatenate([sv_ref[b, :, g * hd:(g + 1) * hd],
                                  kvn_ref[b, :, v0 + g * hd: v0 + (g + 1) * hd]], axis=0).astype(BF16)
            q = jnp.concatenate([q_ref[b, :, (g * group + r) * hd:(g * group + r + 1) * hd] for r in range(group)],
                                axis=0).astype(BF16)
            s = jnp.where(mask, lax.dot_general(q, kk, NT_DIMS, preferred_element_type=F32), NEG_INF)
            sink = jnp.concatenate([jnp.full((tpad, 1), sinks_ref[g * group + r], F32) for r in range(group)], axis=0)
            o = jnp.dot(_sink_softmax(s, sink).astype(BF16), vv, preferred_element_type=F32)
            for r in range(group):
                h = g * group + r
                o_ref[b, :, h * hd:(h + 1) * hd] = o[r * tpad:(r + 1) * tpad].astype(o_ref.dtype)
        return carry

    lax.fori_loop(0, seqs, one_seq, 0)


def _swa_sample_attn(q, kv_new, state_k, state_v, sinks, kv_heads, group, tokens):
    dec_batch, tpad, nq = q.shape
    window = state_k.shape[1]
    seqs = 8
    blk = lambda b: (b, 0, 0)
    return pl.pallas_call(
        functools.partial(_swa_sample_kernel, kv_heads=kv_heads, group=group, window=window, tokens=tokens,
                          seqs=seqs),
        grid=(dec_batch // seqs,),
        in_specs=[pl.BlockSpec(memory_space=pltpu.SMEM),
                  pl.BlockSpec((seqs, tpad, nq), blk), pl.BlockSpec((seqs, tpad, kv_new.shape[2]), blk),
                  pl.BlockSpec((seqs, window, state_k.shape[2]), blk),
                  pl.BlockSpec((seqs, window, state_v.shape[2]), blk)],
        out_specs=pl.BlockSpec((seqs, tpad, nq), blk),
        out_shape=jax.ShapeDtypeStruct((dec_batch, tpad, nq), F32),
        compiler_params=_cparams("parallel"),
        name="swa_sample_attn",
    )(sinks, q, kv_new, state_k, state_v)


def _mla_layer(xp, xs, dims, cache_lat, cache_rope, layer, page_table, g_mix, w_dqkv, g_qa, g_kva, w_uq, w_ukv,
               g_qn, g_kn, w_o, tabs_p, tabs_s):
    batch, seq, dec_batch, tokens = dims
    q_lora, n_heads, qk_dim = w_uq.shape
    kv_lora = w_ukv.shape[0]
    nope = qk_dim - ROPE_DIM
    vdim = w_ukv.shape[2] - nope
    n_pages = page_table.shape[1]
    page = cache_lat.shape[2]
    scale = float(qk_dim) ** -0.5

    wq = w_dqkv[:, :q_lora].astype(BF16)
    wkv = w_dqkv[:, q_lora:q_lora + kv_lora].astype(BF16)
    wpe = w_dqkv[:, q_lora + kv_lora:].astype(BF16)
    wpe2 = jnp.concatenate([wpe, wpe], axis=1)
    w_uq_r = jnp.concatenate([w_uq[:, :, :nope].reshape(q_lora, -1), w_uq[:, :, nope:].reshape(q_lora, -1)],
                             axis=1).astype(BF16)
    w_ukv_r = jnp.concatenate([w_ukv[:, :, :nope].reshape(kv_lora, -1), w_ukv[:, :, nope:].reshape(kv_lora, -1)],
                              axis=1).astype(BF16)
    wukt3 = jnp.transpose(w_ukv[:, :, :nope], (1, 2, 0))
    wukt = wukt3.reshape(n_heads * nope, kv_lora).astype(BF16)
    w_absorb = (wukt3 * g_kn[None, :nope, None]).astype(BF16)
    w_uv = jnp.transpose(w_ukv[:, :, nope:], (1, 0, 2)).astype(BF16)
    w_o2 = w_o.reshape(n_heads * vdim, -1).astype(BF16)
    row = lambda v: v.reshape(1, -1)
    pair = lambda v: jnp.tile(v, 2).reshape(1, -1)
    gq_n, gq_p = row(g_qn[:nope]), pair(g_qn[nope:])
    gk_n, gk_p = row(g_kn[:nope]), pair(g_kn[nope:])

    cq_p, c_p, kpe2_p = _mla_down(xp, row(g_mix), wq, wkv, wpe2, row(g_qa), row(g_kva))
    q_p = _mla_q(cq_p, w_uq_r, gq_n, gq_p, *tabs_p, n_heads, nope, scale, BF16)
    k_p, v_p = _mla_kv(c_p, kpe2_p, w_ukv_r, gk_n, gk_p, *tabs_p, n_heads, nope, vdim)
    o_p = _mla_flash(q_p, k_p, v_p, batch, seq)
    xp = _proj_res(o_p, w_o2, xp)

    cq_s, c_s, kpe2_s = _mla_down(xs, row(g_mix), wq, wkv, wpe2, row(g_qa), row(g_kva))
    q_s = _mla_q(cq_s, w_uq_r, gq_n, gq_p, *tabs_s, n_heads, nope, scale, F32)
    qabs = _bmm(q_s[:, :, :nope].astype(BF16), w_absorb, BF16)
    per_seq = lambda a: jnp.transpose(a.reshape(n_heads, dec_batch, tokens, -1), (1, 2, 0, 3)).reshape(
        dec_batch, tokens * n_heads, -1)
    qabs = per_seq(qabs)
    qpe = per_seq(q_s[:, :, nope:nope + ROPE_DIM])
    qpes = jnp.concatenate([qpe[..., ROPE_DIM // 2:], qpe[..., :ROPE_DIM // 2]], axis=-1)
    new_rows = 16
    pad_new = lambda a: jnp.pad(a.reshape(dec_batch, tokens, -1), ((0, 0), (0, new_rows - tokens), (0, 0)))
    kpe_s = kpe2_s[:, :ROPE_DIM]
    half = ROPE_DIM // 2
    inv_freq = ROPE_THETA ** (-(jnp.arange(half, dtype=F32) * 2.0) / ROPE_DIM)
    ang_in = inv_freq[:, None] * jnp.arange(page, dtype=F32)[None, :]
    cos_t = jnp.tile(jnp.cos(ang_in), (2, 1))
    sin_t = jnp.concatenate([-jnp.sin(ang_in), jnp.sin(ang_in)], axis=0)
    ang_pg = (jnp.arange(n_pages + 1, dtype=F32) * page)[:, None] * inv_freq[None, :]
    pcos = jnp.tile(jnp.cos(ang_pg), (1, 2))
    psin = jnp.concatenate([jnp.sin(ang_pg), -jnp.sin(ang_pg)], axis=1)
    o_lat = _mla_sample_attn(page_table, cache_lat, cache_rope, layer, wukt, qabs, qpe, qpes,
                             pad_new(c_s), pad_new(kpe_s), cos_t, sin_t, pcos, psin,
                             g_kn[nope:].reshape(-1, 1), n_heads, nope, tokens)
    o_lat = jnp.transpose(o_lat.reshape(dec_batch, tokens, n_heads, kv_lora), (2, 0, 1, 3)).reshape(
        n_heads, dec_batch * tokens, kv_lora).astype(BF16)
    o_s = _bmm(o_lat, w_uv, BF16)
    o_s = jnp.transpose(o_s, (1, 0, 2)).reshape(dec_batch * tokens, n_heads * vdim)
    xs = _proj_res(o_s, w_o2, xs)

    outs = (c_p.reshape(batch, seq, kv_lora), kpe2_p[:, :ROPE_DIM].reshape(batch, seq, ROPE_DIM),
            c_s.reshape(dec_batch, tokens, kv_lora), kpe_s.reshape(dec_batch, tokens, ROPE_DIM))
    return xp, xs, outs


def _swa_layer(xp, xs, dims, state_k, state_v, g_mix, w_qkv, b_qkv, g_qn, g_kn, sinks, w_o, b_o, tabs_p, tabs_s):
    batch, seq, dec_batch, tokens = dims
    window, kv_heads, hd = state_k.shape[1], state_k.shape[2], state_k.shape[3]
    n_heads = sinks.shape[0]
    group = n_heads // kv_heads
    nq, nk = n_heads * hd, kv_heads * hd
    scale = float(hd) ** -0.5
    row = lambda v: v.reshape(1, -1)
    pair = lambda v: jnp.tile(v, 2).reshape(1, -1)
    w = w_qkv.astype(BF16)
    wq, wkv = w[:, :nq], w[:, nq:]
    bq, bkv = row(b_qkv[:nq]), row(b_qkv[nq:])
    w_o = w_o.astype(BF16)

    def project(x, tabs, q_dtype):
        q = _swa_proj(x, row(g_mix), wq, bq, pair(g_qn), *tabs, 1024 // LANES, scale, q_dtype)
        kv = _swa_proj(x, row(g_mix), wkv, bkv, pair(g_kn), *tabs, nk // LANES, 1.0, F32)
        return q, kv

    q_p, kv_p = project(xp, tabs_p, BF16)
    o_p = _swa_prompt_attn(q_p, kv_p, sinks, batch, seq, kv_heads, group, window)
    xp = _proj_res(o_p, w_o, xp, row(b_o))

    q_s, kv_s = project(xs, tabs_s, F32)
    tpad = 8
    pad_t = lambda a: jnp.pad(a.reshape(dec_batch, tokens, -1), ((0, 0), (0, tpad - tokens), (0, 0)))
    sk = state_k.reshape(dec_batch, window, nk)
    sv = state_v.reshape(dec_batch, window, nk)
    o_s = _swa_sample_attn(pad_t(q_s), pad_t(kv_s), sk, sv, sinks, kv_heads, group, tokens)
    xs = _proj_res(o_s[:, :tokens].reshape(dec_batch * tokens, nq).astype(BF16), w_o, xs, row(b_o))

    kv_p4 = kv_p.reshape(batch, seq, 2, kv_heads, hd)[:, seq - window:]
    kv_s4 = kv_s.reshape(dec_batch, tokens, 2, kv_heads, hd)
    new_k = jnp.concatenate([state_k, kv_s4[:, :, 0]], axis=1)[:, -window:]
    new_v = jnp.concatenate([state_v, kv_s4[:, :, 1]], axis=1)[:, -window:]
    return xp, xs, (kv_p4[:, :, 0], kv_p4[:, :, 1], new_k, new_v)


def kernel(x_prompt, x_sample, cache_mla_latent, cache_mla_rope, page_table, state_swa_k, state_swa_v, norm_mix, norm_mlp, mla_w_dqkv, mla_g_qa, mla_g_kva, mla_w_uq, mla_w_ukv, mla_g_qn, mla_g_kn, mla_w_o, swa_w_qkv, swa_b_qkv, swa_g_qn, swa_g_kn, swa_sinks, swa_w_o, swa_b_o, mlp_w_up, mlp_w_down):
    batch, seq, d_model = x_prompt.shape
    dec_batch, tokens, _ = x_sample.shape
    depth = norm_mix.shape[0]
    past_len = page_table.shape[1] * cache_mla_latent.shape[2]
    dims = (batch, seq, dec_batch, tokens)
    tabs_p = _rope_tables(jnp.arange(seq))
    tabs_s = _rope_tables(past_len + jnp.arange(dec_batch * tokens) % tokens)
    xp = x_prompt.reshape(batch * seq, d_model)
    xs = x_sample.reshape(dec_batch * tokens, d_model)
    mla_outs, swa_outs = [], []
    for i in range(depth):
        j = i // 2
        if i % 2 == 0:
            xp, xs, outs = _mla_layer(xp, xs, dims, cache_mla_latent, cache_mla_rope, j, page_table, norm_mix[i],
                                      mla_w_dqkv[j], mla_g_qa[j], mla_g_kva[j], mla_w_uq[j], mla_w_ukv[j],
                                      mla_g_qn[j], mla_g_kn[j], mla_w_o[j], tabs_p, tabs_s)
            mla_outs.append(outs)
        else:
            xp, xs, outs = _swa_layer(xp, xs, dims, state_swa_k[j], state_swa_v[j], norm_mix[i], swa_w_qkv[j],
                                      swa_b_qkv[j], swa_g_qn[j], swa_g_kn[j], swa_sinks[j], swa_w_o[j], swa_b_o[j],
                                      tabs_p, tabs_s)
            swa_outs.append(outs)
        w_up, w_down = mlp_w_up[i].astype(BF16), mlp_w_down[i].astype(BF16)
        xp = _mlp(xp, norm_mlp[i].reshape(1, -1), w_up, w_down)
        xs = _mlp(xs, norm_mlp[i].reshape(1, -1), w_up, w_down)
    stack = lambda outs, k: jnp.stack([o[k] for o in outs])
    return (xp.reshape(batch, seq, d_model), xs.reshape(dec_batch, tokens, d_model),
            stack(mla_outs, 0), stack(mla_outs, 1), stack(mla_outs, 2), stack(mla_outs, 3),
            stack(swa_outs, 0), stack(swa_outs, 1), stack(swa_outs, 2), stack(swa_outs, 3))
```
